```python
import jax, jax.numpy as jnp
from jax import lax
import numpy as np

D_MODEL = 1024
BATCH = 1
SEQ = 16384
DEPTH = 1

D_MIX = D_MODEL
D_LRU = D_MIX // 2
LRU_HEADS = 8
LRU_HEAD_DIM = D_LRU // LRU_HEADS
LRU_CONV = 4
LRU_C = 8.0
D_POOL = D_MIX - D_LRU
POOL_WINDOWS = (2, 4, 8, 16)
POOL_GROUPS = len(POOL_WINDOWS)
POOL_GROUP_DIM = D_POOL // POOL_GROUPS
D_IN = 2 * D_LRU + D_POOL
D_FF = 3 * D_MODEL
FFN_CONV = 3
D_PLE = 256
EPS = 1e-6

kernel_name = 'hybrid_rglru_pool_block'


def rms_norm(x, g):
    xf = x.astype(jnp.float32)
    y = xf * lax.rsqrt(jnp.mean(xf * xf, axis=-1, keepdims=True) + EPS)
    return (y * g.astype(jnp.float32)).astype(x.dtype)


def causal_dwconv(x, w, b):
    k, c = w.shape
    y = lax.conv_general_dilated(x, w[:, None, :].astype(x.dtype), window_strides=(1,),
                                 padding=[(k - 1, 0)],
                                 dimension_numbers=('NWC', 'WIO', 'NWC'),
                                 feature_group_count=c)
    return y + b.astype(x.dtype)


def _lin_rec_combine(c1, c2):
    a1, b1 = c1
    a2, b2 = c2
    return a1 * a2, a2 * b1 + b2


def rg_lru(x, w_a, b_a, w_x, b_x, lam):
    bsz, s, _ = x.shape
    xf = x.astype(jnp.float32)
    xh = xf.reshape(bsz, s, LRU_HEADS, LRU_HEAD_DIM)
    r = jax.nn.sigmoid(jnp.einsum('bshi,hij->bshj', xh, w_a.astype(jnp.float32)).reshape(bsz, s, D_LRU) + b_a.astype(jnp.float32))
    i = jax.nn.sigmoid(jnp.einsum('bshi,hij->bshj', xh, w_x.astype(jnp.float32)).reshape(bsz, s, D_LRU) + b_x.astype(jnp.float32))
    log_a = -LRU_C * r * jax.nn.softplus(-lam.astype(jnp.float32))
    a = jnp.exp(log_a)
    mult = jnp.sqrt(-jnp.expm1(2.0 * log_a))
    _, h = lax.associative_scan(_lin_rec_combine, (a, mult * (i * xf)), axis=1)
    return h.astype(x.dtype)


def multiscale_pool(x, w_pool, b_pool, pool_scale):
    bsz, s, _ = x.shape
    xg = x.astype(jnp.float32).reshape(bsz, s, POOL_GROUPS, POOL_GROUP_DIM)
    cs = jnp.cumsum(xg, axis=1)
    count = jnp.arange(1, s + 1, dtype=jnp.float32)[None, :, None]
    outs = []
    for g, w in enumerate(POOL_WINDOWS):
        c = cs[:, :, g]
        lag = jnp.pad(c, ((0, 0), (w, 0), (0, 0)))[:, :s]
        mean = (c - lag) / jnp.minimum(count, float(w))
        outs.append(mean - xg[:, :, g])
    pooled = jnp.stack(outs, axis=2)
    y = jnp.einsum('bsgc,gcd->bsgd', pooled, w_pool.astype(jnp.float32)).reshape(bsz, s, D_POOL)
    y = (y + b_pool.astype(jnp.float32)) * pool_scale.astype(jnp.float32)
    return y.astype(x.dtype)


def setup_inputs(seed: int = 0) -> dict:
    key = jax.random.key(seed)
    ks = jax.random.split(key, 26)
    f32 = jnp.float32

    def nrm(k, shape, fan_in):
        return jax.random.normal(k, shape, f32) * (fan_in ** -0.5)

    def gain(k, shape):
        return 1.0 + 0.05 * jax.random.normal(k, shape, f32)

    def small(k, shape):
        return 0.02 * jax.random.normal(k, shape, f32)

    a_target = jax.random.uniform(ks[9], (DEPTH, D_LRU), f32, 0.9, 0.999)
    base = a_target ** (1.0 / LRU_C)
    lam = jnp.log(base) - jnp.log1p(-base)

    return {
        'x': jax.random.normal(ks[0], (BATCH, SEQ, D_MODEL), f32),
        'p': jax.random.normal(ks[1], (DEPTH, BATCH, SEQ, D_PLE), f32),
        'g_mix': gain(ks[2], (DEPTH, D_MODEL)),
        'w_in': nrm(ks[3], (DEPTH, D_MODEL, D_IN), D_MODEL),
        'w_conv_lru': nrm(ks[4], (DEPTH, LRU_CONV, D_LRU), LRU_CONV),
        'b_conv_lru': small(ks[5], (DEPTH, D_LRU)),
        'w_a': nrm(ks[6], (DEPTH, LRU_HEADS, LRU_HEAD_DIM, LRU_HEAD_DIM), LRU_HEAD_DIM),
        'b_a': small(ks[7], (DEPTH, D_LRU)),
        'w_x': nrm(ks[8], (DEPTH, LRU_HEADS, LRU_HEAD_DIM, LRU_HEAD_DIM), LRU_HEAD_DIM),
        'b_x': small(ks[10], (DEPTH, D_LRU)),
        'lam': lam,
        'w_pool': nrm(ks[11], (DEPTH, POOL_GROUPS, POOL_GROUP_DIM, POOL_GROUP_DIM), POOL_GROUP_DIM),
        'b_pool': small(ks[12], (DEPTH, D_POOL)),
        'pool_scale': gain(ks[13], (DEPTH, D_POOL)),
        'w_out': nrm(ks[14], (DEPTH, D_MIX, D_MODEL), D_MIX),
        'g_ffn': gain(ks[15], (DEPTH, D_MODEL)),
        'w_up': nrm(ks[16], (DEPTH, D_MODEL, 2 * D_FF), D_MODEL),
        'w_conv_ffn': nrm(ks[17], (DEPTH, FFN_CONV, 2 * D_FF), FFN_CONV),
        'b_conv_ffn': small(ks[18], (DEPTH, 2 * D_FF)),
        'w_down': nrm(ks[19], (DEPTH, D_FF, D_MODEL), D_FF),
        'g_ple': gain(ks[20], (DEPTH, D_MODEL)),
        'w_ple_gate': nrm(ks[21], (DEPTH, D_MODEL, D_MODEL), D_MODEL),
        'b_ple_gate': small(ks[22], (DEPTH, D_MODEL)),
        'w_ple_proj': nrm(ks[23], (DEPTH, D_PLE, D_MODEL), D_PLE),
        'g_final': gain(ks[24], (D_MODEL,)),
    }


def reference(x, p, g_mix, w_in, w_conv_lru, b_conv_lru, w_a, b_a, w_x, b_x, lam,
              w_pool, b_pool, pool_scale, w_out, g_ffn, w_up, w_conv_ffn, b_conv_ffn,
              w_down, g_ple, w_ple_gate, b_ple_gate, w_ple_proj, g_final):
    h = x
    for i in range(DEPTH):
        u = rms_norm(h, g_mix[i])
        z = u @ w_in[i]
        z_lru = z[..., :D_LRU]
        z_gate = z[..., D_LRU:2 * D_LRU]
        z_pool = z[..., 2 * D_LRU:]
        xc = causal_dwconv(z_lru, w_conv_lru[i], b_conv_lru[i])
        y_lru = rg_lru(xc, w_a[i], b_a[i], w_x[i], b_x[i], lam[i]) * jax.nn.gelu(z_gate)
        y_pool = multiscale_pool(z_pool, w_pool[i], b_pool[i], pool_scale[i])
        h = h + jnp.concatenate([y_lru, y_pool], axis=-1) @ w_out[i]
        v = rms_norm(h, g_ffn[i])
        up = causal_dwconv(v @ w_up[i], w_conv_ffn[i], b_conv_ffn[i])
        gate = up[..., :D_FF]
        val = up[..., D_FF:]
        h = h + (jax.nn.gelu(gate) * val) @ w_down[i]
        q = rms_norm(h, g_ple[i])
        h = h + jax.nn.sigmoid(q @ w_ple_gate[i] + b_ple_gate[i]) * (p[i] @ w_ple_proj[i])
    return rms_norm(h, g_final)
```

```python
import functools
import math

import jax
import jax.numpy as jnp
from jax import lax
from jax.experimental import pallas as pl
from jax.experimental.pallas import tpu as pltpu

D_MODEL = 1024
D_LRU = 512
LRU_HEADS = 8
LRU_HEAD_DIM = 64
LRU_CONV = 4
LRU_C = 8.0
D_POOL = 512
POOL_WINDOWS = (2, 4, 8, 16)
POOL_GROUP_DIM = 128
D_IN = 2 * D_LRU + D_POOL
D_FF = 3 * D_MODEL
FFN_CONV = 3
D_PLE = 256
EPS = 1e-6

SUBLANES = 8
MXU_DIM = 256
SEQ_TILE = 512
FF_CHUNK = 512
POOL_HIST = 16
VMEM_LIMIT_BYTES = 56 * 1024 * 1024

_GELU_C = math.sqrt(2.0 / math.pi)


def _rms_norm(x, g):
    ms = jnp.mean(x * x, axis=-1, keepdims=True)
    return x * lax.rsqrt(ms + EPS) * g


def _gelu_tanh(x):
    inner = x * (_GELU_C + (_GELU_C * 0.044715) * (x * x))
    hx = 0.5 * x
    return hx + hx * jnp.tanh(inner)


def _sigmoid(x):
    return 1.0 / (1.0 + jnp.exp(-x))


def _softplus(x):
    return jnp.maximum(x, 0.0) + jnp.log1p(jnp.exp(-jnp.abs(x)))


def _shift_rows(x, k, tail):
    xr = pltpu.roll(x, k, 0)
    tr = pltpu.roll(tail, k, 0)
    row = lax.broadcasted_iota(jnp.int32, tail.shape, 0)
    head = jnp.where(row < k, tr, xr[:SUBLANES])
    return jnp.concatenate([head, xr[SUBLANES:]], axis=0)


def _bf16_dot(a, b):
    return jnp.dot(a.astype(jnp.bfloat16), b, preferred_element_type=jnp.float32)


def _block_kernel(
    x_ref, p_ref, g_mix_ref, w_in_ref, w_conv_lru_ref, b_conv_lru_ref, wa_ref, b_a_ref,
    wx_ref, b_x_ref, lam_ref, w_pool_ref, b_pool_ref, pool_scale_ref, w_out_ref,
    g_ffn_ref, w_up_ref, w_conv_ffn_ref, b_conv_ffn_ref, w_down_ref, g_ple_ref,
    w_ple_gate_ref, b_ple_gate_ref, w_ple_proj_ref, g_final_ref,
    out_ref,
    zl_tail, zp_tail, h_state, up_tail, a_buf, b_buf, h_buf, acc_ref,
):
    ts = x_ref.shape[0]
    n_groups = ts // SUBLANES
    n_chunks = w_up_ref.shape[0]
    pid = pl.program_id(0)

    @pl.when(pid == 0)
    def _():
        zl_tail[...] = jnp.zeros_like(zl_tail)
        zp_tail[...] = jnp.zeros_like(zp_tail)
        h_state[...] = jnp.zeros_like(h_state)
        up_tail[...] = jnp.zeros_like(up_tail)

    x = x_ref[...]

    u = _rms_norm(x, g_mix_ref[...])
    z = _bf16_dot(u, w_in_ref[...])
    z_lru = z[:, :D_LRU]
    z_gate = z[:, D_LRU:2 * D_LRU]
    z_pool = z[:, 2 * D_LRU:]

    tail = zl_tail[...]
    wc = w_conv_lru_ref[...]
    xc = z_lru * wc[LRU_CONV - 1:LRU_CONV] + b_conv_lru_ref[...]
    for k in range(1, LRU_CONV):
        xc = xc + _shift_rows(z_lru, k, tail) * wc[LRU_CONV - 1 - k:LRU_CONV - k]
    zl_tail[...] = z_lru[ts - SUBLANES:]

    xc_bf = xc.astype(jnp.bfloat16)
    r_parts, i_parts = [], []
    for c in range(D_LRU // MXU_DIM):
        blk = xc_bf[:, c * MXU_DIM:(c + 1) * MXU_DIM]
        r_parts.append(jnp.dot(blk, wa_ref[c], preferred_element_type=jnp.float32))
        i_parts.append(jnp.dot(blk, wx_ref[c], preferred_element_type=jnp.float32))
    r = _sigmoid(jnp.concatenate(r_parts, axis=-1) + b_a_ref[...])
    i_gate = _sigmoid(jnp.concatenate(i_parts, axis=-1) + b_x_ref[...])

    log_a = (-LRU_C * _softplus(-lam_ref[...])) * r
    a = jnp.exp(log_a)
    mult = jnp.sqrt(1.0 - a * a)
    b = mult * (i_gate * xc)

    a3 = a.reshape(n_groups, SUBLANES, D_LRU)
    b3 = b.reshape(n_groups, SUBLANES, D_LRU)
    sub = lax.broadcasted_iota(jnp.int32, (n_groups, SUBLANES, D_LRU), 1)
    d = 1
    while d < SUBLANES:
        keep = sub >= d
        a_s = jnp.where(keep, pltpu.roll(a3, d, 1), 1.0)
        b_s = jnp.where(keep, pltpu.roll(b3, d, 1), 0.0)
        b3 = a3 * b_s + b3
        a3 = a3 * a_s
        d *= 2
    a_buf[...] = a3.reshape(ts, D_LRU)
    b_buf[...] = b3.reshape(ts, D_LRU)

    def group_step(g, h_prev):
        rows = pl.ds(pl.multiple_of(g * SUBLANES, SUBLANES), SUBLANES)
        h_last = jnp.broadcast_to(h_prev[SUBLANES - 1:SUBLANES], (SUBLANES, D_LRU))
        h_new = a_buf[rows, :] * h_last + b_buf[rows, :]
        h_buf[rows, :] = h_new
        return h_new

    h_state[...] = lax.fori_loop(0, n_groups, group_step, h_state[...], unroll=8)
    y_lru = h_buf[...] * _gelu_tanh(z_gate)

    hist = jnp.concatenate([zp_tail[...], z_pool], axis=0)
    zp_tail[...] = z_pool[ts - POOL_HIST:]
    t_glob = pid * ts + lax.broadcasted_iota(jnp.int32, (ts, 1), 0)
    pooled = []
    s = hist
    for gi, w in enumerate(POOL_WINDOWS):
        lo = gi * POOL_GROUP_DIM
        s = s[:, POOL_GROUP_DIM:] if gi > 0 else s
        s = s + pltpu.roll(s, w // 2, 0)
        cnt = jnp.minimum(t_glob + 1, w).astype(jnp.float32)
        win = s[POOL_HIST:, :POOL_GROUP_DIM]
        pooled.append(win * (1.0 / cnt) - z_pool[:, lo:lo + POOL_GROUP_DIM])
    pooled = jnp.concatenate(pooled, axis=-1).astype(jnp.bfloat16)
    yp_parts = []
    for c in range(D_POOL // MXU_DIM):
        yp_parts.append(jnp.dot(pooled[:, c * MXU_DIM:(c + 1) * MXU_DIM], w_pool_ref[c],
                                preferred_element_type=jnp.float32))
    y_pool = (jnp.concatenate(yp_parts, axis=-1) + b_pool_ref[...]) * pool_scale_ref[...]

    mix = jnp.concatenate([y_lru, y_pool], axis=-1)
    h1 = x + _bf16_dot(mix, w_out_ref[...])

    v = _rms_norm(h1, g_ffn_ref[...]).astype(jnp.bfloat16)
    acc_ref[...] = h1

    def ffn_chunk(j, carry):
        up = jnp.dot(v, w_up_ref[j], preferred_element_type=jnp.float32)
        tail_j = up_tail[j]
        wf = w_conv_ffn_ref[j]
        cv = up * wf[FFN_CONV - 1:FFN_CONV] + b_conv_ffn_ref[j]
        for k in range(1, FFN_CONV):
            cv = cv + _shift_rows(up, k, tail_j) * wf[FFN_CONV - 1 - k:FFN_CONV - k]
        up_tail[j] = up[ts - SUBLANES:]
        act = _gelu_tanh(cv[:, :FF_CHUNK]) * cv[:, FF_CHUNK:]
        acc_ref[...] += _bf16_dot(act, w_down_ref[j])
        return carry

    lax.fori_loop(0, n_chunks, ffn_chunk, 0)
    h2 = acc_ref[...]

    q = _rms_norm(h2, g_ple_ref[...])
    gate = _sigmoid(_bf16_dot(q, w_ple_gate_ref[...]) + b_ple_gate_ref[...])
    pe = _bf16_dot(p_ref[...], w_ple_proj_ref[...])
    h3 = h2 + gate * pe
    out_ref[...] = _rms_norm(h3, g_final_ref[...])


def _block_diag(w, per_block):
    h, d, _ = w.shape
    w = w.reshape(h // per_block, per_block, d, d)
    eye = jnp.eye(per_block, dtype=w.dtype)
    out = jnp.einsum('bpij,pq->bpiqj', w, eye)
    return out.reshape(h // per_block, per_block * d, per_block * d)


def _resident(shape):
    nd = len(shape)
    return pl.BlockSpec(shape, lambda i: (0,) * nd, pipeline_mode=pl.Buffered(1))


def kernel(x, p, g_mix, w_in, w_conv_lru, b_conv_lru, w_a, b_a, w_x, b_x, lam, w_pool, b_pool, pool_scale, w_out, g_ffn, w_up, w_conv_ffn, b_conv_ffn, w_down, g_ple, w_ple_gate, b_ple_gate, w_ple_proj, g_final):
    bsz, seq, d_model = x.shape
    assert bsz == 1 and d_model == D_MODEL and seq % SEQ_TILE == 0
    assert g_mix.shape[0] == 1
    bf16 = jnp.bfloat16
    n_chunks = D_FF // FF_CHUNK
    row = lambda a: a.reshape(1, -1)

    def chunked_cols(a):
        lead = a.shape[:-1]
        a = a.reshape(lead + (2, n_chunks, FF_CHUNK))
        a = jnp.moveaxis(a, -2, 0)
        return a.reshape((n_chunks,) + lead + (2 * FF_CHUNK,))

    heads_per_blk = MXU_DIM // LRU_HEAD_DIM
    groups_per_blk = MXU_DIM // POOL_GROUP_DIM
    operands = [
        x[0], p[0, 0], row(g_mix[0]), w_in[0].astype(bf16), w_conv_lru[0], row(b_conv_lru[0]),
        _block_diag(w_a[0], heads_per_blk).astype(bf16), row(b_a[0]),
        _block_diag(w_x[0], heads_per_blk).astype(bf16), row(b_x[0]), row(lam[0]),
        _block_diag(w_pool[0], groups_per_blk).astype(bf16), row(b_pool[0]), row(pool_scale[0]),
        w_out[0].astype(bf16), row(g_ffn[0]),
        chunked_cols(w_up[0]).astype(bf16), chunked_cols(w_conv_ffn[0]),
        chunked_cols(row(b_conv_ffn[0])),
        w_down[0].reshape(n_chunks, FF_CHUNK, D_MODEL).astype(bf16), row(g_ple[0]),
        w_ple_gate[0].astype(bf16), row(b_ple_gate[0]), w_ple_proj[0].astype(bf16), row(g_final),
    ]
    in_specs = [
        pl.BlockSpec((SEQ_TILE, D_MODEL), lambda i: (i, 0)),
        pl.BlockSpec((SEQ_TILE, D_PLE), lambda i: (i, 0)),
    ] + [_resident(a.shape) for a in operands[2:]]
    f32 = jnp.float32
    scratch = [
        pltpu.VMEM((SUBLANES, D_LRU), f32),
        pltpu.VMEM((POOL_HIST, D_POOL), f32),
        pltpu.VMEM((SUBLANES, D_LRU), f32),
        pltpu.VMEM((n_chunks, SUBLANES, 2 * FF_CHUNK), f32),
        pltpu.VMEM((SEQ_TILE, D_LRU), f32),
        pltpu.VMEM((SEQ_TILE, D_LRU), f32),
        pltpu.VMEM((SEQ_TILE, D_LRU), f32),
        pltpu.VMEM((SEQ_TILE, D_MODEL), f32),
    ]
    out = pl.pallas_call(
        _block_kernel,
        grid=(seq // SEQ_TILE,),
        in_specs=in_specs,
        out_specs=pl.BlockSpec((SEQ_TILE, D_MODEL), lambda i: (i, 0)),
        out_shape=jax.ShapeDtypeStruct((seq, D_MODEL), x.dtype),
        scratch_shapes=scratch,
        compiler_params=pltpu.CompilerParams(
            dimension_semantics=("arbitrary",),
            vmem_limit_bytes=VMEM_LIMIT_BYTES,
        ),
        name="hybrid_block",
    )(*operands)
    return out[None]
```

```python
import math

import jax
import jax.numpy as jnp
from jax import lax
from jax.experimental import pallas as pl
from jax.experimental.pallas import tpu as pltpu

D_MODEL = 1024
D_LRU = 512
LRU_HEADS = 8
LRU_HEAD_DIM = 64
LRU_CONV = 4
LRU_C = 8.0
D_POOL = 512
POOL_WINDOWS = (2, 4, 8, 16)
POOL_GROUP_DIM = 128
D_IN = 2 * D_LRU + D_POOL
D_FF = 3 * D_MODEL
FFN_CONV = 3
D_PLE = 256
EPS = 1e-6

SUBLANES = 8
MXU_DIM = 256
SEQ_TILE = 512
FF_CHUNK = 512
POOL_HIST = 16
VMEM_LIMIT_BYTES = 56 * 1024 * 1024

_GELU_C = math.sqrt(2.0 / math.pi)


def _rms_norm(x, g):
    ms = jnp.mean(x * x, axis=-1, keepdims=True)
    return x * lax.rsqrt(ms + EPS) * g


def _gelu_tanh(x):
    inner = x * (_GELU_C + (_GELU_C * 0.044715) * (x * x))
    hx = 0.5 * x
    return hx + hx * jnp.tanh(inner)


def _sigmoid(x):
    return 1.0 / (1.0 + jnp.exp(-x))


def _softplus(x):
    return jnp.maximum(x, 0.0) + jnp.log1p(jnp.exp(-jnp.abs(x)))


def _shift_rows(x, k, tail):
    xr = pltpu.roll(x, k, 0)
    tr = pltpu.roll(tail, k, 0)
    row = lax.broadcasted_iota(jnp.int32, tail.shape, 0)
    head = jnp.where(row < k, tr, xr[:SUBLANES])
    return jnp.concatenate([head, xr[SUBLANES:]], axis=0)


def _bf16_dot(a, b):
    return jnp.dot(a.astype(jnp.bfloat16), b, preferred_element_type=jnp.float32)


def _block_kernel(
    x_ref, p_ref, g_mix_ref, w_in_ref, w_conv_lru_ref, b_conv_lru_ref, wa_ref, b_a_ref,
    wx_ref, b_x_ref, lam_ref, w_pool_ref, b_pool_ref, pool_scale_ref, w_out_ref,
    g_ffn_ref, w_up_ref, w_conv_ffn_ref, b_conv_ffn_ref, w_down_ref, g_ple_ref,
    w_ple_gate_ref, b_ple_gate_ref, w_ple_proj_ref, g_final_ref,
    out_ref,
    zl_tail, zp_tail, h_state, up_tail, a_buf, b_buf, h_buf, acc_ref, v_buf, up_buf, act_buf,
):
    ts = x_ref.shape[0]
    n_groups = ts // SUBLANES
    pid = pl.program_id(0)

    @pl.when(pid == 0)
    def _():
        zl_tail[...] = jnp.zeros_like(zl_tail)
        zp_tail[...] = jnp.zeros_like(zp_tail)
        h_state[...] = jnp.zeros_like(h_state)
        up_tail[...] = jnp.zeros_like(up_tail)

    x = x_ref[...]

    u = _rms_norm(x, g_mix_ref[...])
    z = _bf16_dot(u, w_in_ref[...])
    z_lru = z[:, :D_LRU]
    z_gate = z[:, D_LRU:2 * D_LRU]
    z_pool = z[:, 2 * D_LRU:]

    tail = zl_tail[...]
    wc = w_conv_lru_ref[...]
    xc = z_lru * wc[LRU_CONV - 1:LRU_CONV] + b_conv_lru_ref[...]
    for k in range(1, LRU_CONV):
        xc = xc + _shift_rows(z_lru, k, tail) * wc[LRU_CONV - 1 - k:LRU_CONV - k]
    zl_tail[...] = z_lru[ts - SUBLANES:]

    xc_bf = xc.astype(jnp.bfloat16)
    r_parts, i_parts = [], []
    for c in range(D_LRU // MXU_DIM):
        blk = xc_bf[:, c * MXU_DIM:(c + 1) * MXU_DIM]
        r_parts.append(jnp.dot(blk, wa_ref[c], preferred_element_type=jnp.float32))
        i_parts.append(jnp.dot(blk, wx_ref[c], preferred_element_type=jnp.float32))
    r = _sigmoid(jnp.concatenate(r_parts, axis=-1) + b_a_ref[...])
    i_gate = _sigmoid(jnp.concatenate(i_parts, axis=-1) + b_x_ref[...])

    log_a = (-LRU_C * _softplus(-lam_ref[...])) * r
    a = jnp.exp(log_a)
    mult = jnp.sqrt(1.0 - a * a)
    b = mult * (i_gate * xc)

    a3 = a.reshape(n_groups, SUBLANES, D_LRU)
    b3 = b.reshape(n_groups, SUBLANES, D_LRU)
    sub = lax.broadcasted_iota(jnp.int32, (n_groups, SUBLANES, D_LRU), 1)
    d = 1
    while d < SUBLANES:
        keep = sub >= d
        a_s = jnp.where(keep, pltpu.roll(a3, d, 1), 1.0)
        b_s = jnp.where(keep, pltpu.roll(b3, d, 1), 0.0)
        b3 = a3 * b_s + b3
        a3 = a3 * a_s
        d *= 2
    a_buf[...] = a3.reshape(ts, D_LRU)
    b_buf[...] = b3.reshape(ts, D_LRU)

    def group_step(g, h_prev):
        rows = pl.ds(pl.multiple_of(g * SUBLANES, SUBLANES), SUBLANES)
        h_last = jnp.broadcast_to(h_prev[SUBLANES - 1:SUBLANES], (SUBLANES, D_LRU))
        h_new = a_buf[rows, :] * h_last + b_buf[rows, :]
        h_buf[rows, :] = h_new
        return h_new

    h_state[...] = lax.fori_loop(0, n_groups, group_step, h_state[...], unroll=8)
    y_lru = h_buf[...] * _gelu_tanh(z_gate)

    hist = jnp.concatenate([zp_tail[...], z_pool], axis=0)
    zp_tail[...] = z_pool[ts - POOL_HIST:]
    t_glob = pid * ts + lax.broadcasted_iota(jnp.int32, (ts, 1), 0)
    pooled = []
    s = hist
    for gi, w in enumerate(POOL_WINDOWS):
        lo = gi * POOL_GROUP_DIM
        s = s[:, POOL_GROUP_DIM:] if gi > 0 else s
        s = s + pltpu.roll(s, w // 2, 0)
        cnt = jnp.minimum(t_glob + 1, w).astype(jnp.float32)
        win = s[POOL_HIST:, :POOL_GROUP_DIM]
        pooled.append(win * (1.0 / cnt) - z_pool[:, lo:lo + POOL_GROUP_DIM])
    pooled = jnp.concatenate(pooled, axis=-1).astype(jnp.bfloat16)
    yp_parts = []
    for c in range(D_POOL // MXU_DIM):
        yp_parts.append(jnp.dot(pooled[:, c * MXU_DIM:(c + 1) * MXU_DIM], w_pool_ref[c],
                                preferred_element_type=jnp.float32))
    y_pool = (jnp.concatenate(yp_parts, axis=-1) + b_pool_ref[...]) * pool_scale_ref[...]

    mix = jnp.concatenate([y_lru, y_pool], axis=-1)
    h1 = x + _bf16_dot(mix, w_out_ref[...])

    v_buf[...] = _rms_norm(h1, g_ffn_ref[...]).astype(jnp.bfloat16)
    acc_ref[...] = h1

    n_chunks = D_FF // FF_CHUNK

    def up_proj(j):
        for half in range(2):
            cols = slice(half * D_FF + j * FF_CHUNK, half * D_FF + (j + 1) * FF_CHUNK)
            up_buf[j % 2, :, half * FF_CHUNK:(half + 1) * FF_CHUNK] = jnp.dot(
                v_buf[...], w_up_ref[:, cols], preferred_element_type=jnp.float32)

    def down_proj(j):
        acc_ref[...] += jnp.dot(act_buf[j % 2], w_down_ref[j * FF_CHUNK:(j + 1) * FF_CHUNK, :],
                                preferred_element_type=jnp.float32)

    def elementwise(j):
        cvs = []
        for half in range(2):
            cols = slice(half * D_FF + j * FF_CHUNK, half * D_FF + (j + 1) * FF_CHUNK)
            up = up_buf[j % 2, :, half * FF_CHUNK:(half + 1) * FF_CHUNK]
            tail_c = up_tail[:, cols]
            cv = up * w_conv_ffn_ref[FFN_CONV - 1:FFN_CONV, cols] + b_conv_ffn_ref[:, cols]
            for k in range(1, FFN_CONV):
                cv = cv + _shift_rows(up, k, tail_c) * w_conv_ffn_ref[FFN_CONV - 1 - k:FFN_CONV - k, cols]
            up_tail[:, cols] = up[ts - SUBLANES:]
            cvs.append(cv)
        act_buf[j % 2] = (_gelu_tanh(cvs[0]) * cvs[1]).astype(jnp.bfloat16)

    for step in range(-1, n_chunks + 1):
        if step + 1 < n_chunks:
            up_proj(step + 1)
        if 0 <= step - 1:
            down_proj(step - 1)
        if 0 <= step < n_chunks:
            elementwise(step)
    h2 = acc_ref[...]

    q = _rms_norm(h2, g_ple_ref[...])
    gate = _sigmoid(_bf16_dot(q, w_ple_gate_ref[...]) + b_ple_gate_ref[...])
    pe = _bf16_dot(p_ref[...], w_ple_proj_ref[...])
    h3 = h2 + gate * pe
    out_ref[...] = _rms_norm(h3, g_final_ref[...])


def _block_diag(w, per_block):
    h, d, _ = w.shape
    w = w.reshape(h // per_block, per_block, d, d)
    eye = jnp.eye(per_block, dtype=w.dtype)
    out = jnp.einsum('bpij,pq->bpiqj', w, eye)
    return out.reshape(h // per_block, per_block * d, per_block * d)


def _resident(shape):
    nd = len(shape)
    return pl.BlockSpec(shape, lambda i: (0,) * nd, pipeline_mode=pl.Buffered(1))


def kernel(x, p, g_mix, w_in, w_conv_lru, b_conv_lru, w_a, b_a, w_x, b_x, lam, w_pool, b_pool, pool_scale, w_out, g_ffn, w_up, w_conv_ffn, b_conv_ffn, w_down, g_ple, w_ple_gate, b_ple_gate, w_ple_proj, g_final):
    bsz, seq, d_model = x.shape
    assert bsz == 1 and d_model == D_MODEL and seq % SEQ_TILE == 0
    assert g_mix.shape[0] == 1
    bf16 = jnp.bfloat16
    row = lambda a: a.reshape(1, -1)

    heads_per_blk = MXU_DIM // LRU_HEAD_DIM
    groups_per_blk = MXU_DIM // POOL_GROUP_DIM
    operands = [
        x[0], p[0, 0], row(g_mix[0]), w_in[0].astype(bf16), w_conv_lru[0], row(b_conv_lru[0]),
        _block_diag(w_a[0], heads_per_blk).astype(bf16), row(b_a[0]),
        _block_diag(w_x[0], heads_per_blk).astype(bf16), row(b_x[0]), row(lam[0]),
        _block_diag(w_pool[0], groups_per_blk).astype(bf16), row(b_pool[0]), row(pool_scale[0]),
        w_out[0].astype(bf16), row(g_ffn[0]),
        w_up[0].astype(bf16), w_conv_ffn[0], row(b_conv_ffn[0]),
        w_down[0].astype(bf16), row(g_ple[0]),
        w_ple_gate[0].astype(bf16), row(b_ple_gate[0]), w_ple_proj[0].astype(bf16), row(g_final),
    ]
    in_specs = [
        pl.BlockSpec((SEQ_TILE, D_MODEL), lambda i: (i, 0)),
        pl.BlockSpec((SEQ_TILE, D_PLE), lambda i: (i, 0)),
    ] + [_resident(a.shape) for a in operands[2:]]
    f32 = jnp.float32
    scratch = [
        pltpu.VMEM((SUBLANES, D_LRU), f32),
        pltpu.VMEM((POOL_HIST, D_POOL), f32),
        pltpu.VMEM((SUBLANES, D_LRU), f32),
        pltpu.VMEM((SUBLANES, 2 * D_FF), f32),
        pltpu.VMEM((SEQ_TILE, D_LRU), f32),
        pltpu.VMEM((SEQ_TILE, D_LRU), f32),
        pltpu.VMEM((SEQ_TILE, D_LRU), f32),
        pltpu.VMEM((SEQ_TILE, D_MODEL), f32),
        pltpu.VMEM((SEQ_TILE, D_MODEL), bf16),
        pltpu.VMEM((2, SEQ_TILE, 2 * FF_CHUNK), f32),
        pltpu.VMEM((2, SEQ_TILE, FF_CHUNK), bf16),
    ]
    out = pl.pallas_call(
        _block_kernel,
        grid=(seq // SEQ_TILE,),
        in_specs=in_specs,
        out_specs=pl.BlockSpec((SEQ_TILE, D_MODEL), lambda i: (i, 0)),
        out_shape=jax.ShapeDtypeStruct((seq, D_MODEL), x.dtype),
        scratch_shapes=scratch,
        compiler_params=pltpu.CompilerParams(
            dimension_semantics=("arbitrary",),
            vmem_limit_bytes=VMEM_LIMIT_BYTES,
        ),
        name="hybrid_block",
    )(*operands)
    return out[None]
```

```python
import math

import jax
import jax.numpy as jnp
from jax import lax
from jax.experimental import pallas as pl
from jax.experimental.pallas import tpu as pltpu

D_MODEL = 1024
D_LRU = 512
LRU_HEADS = 8
LRU_HEAD_DIM = 64
LRU_CONV = 4
LRU_C = 8.0
D_POOL = 512
POOL_WINDOWS = (2, 4, 8, 16)
POOL_GROUP_DIM = 128
D_IN = 2 * D_LRU + D_POOL
D_FF = 3 * D_MODEL
FFN_CONV = 3
D_PLE = 256
EPS = 1e-6

SUBLANES = 8
MXU_DIM = 256
SEQ_TILE = 512
FF_CHUNK = 512
EW_ROWS = 512
POOL_HIST = 16
VMEM_LIMIT_BYTES = 62 * 1024 * 1024

_GELU_C = math.sqrt(2.0 / math.pi)


def _rms_norm(x, g):
    ms = jnp.mean(x * x, axis=-1, keepdims=True)
    return x * lax.rsqrt(ms + EPS) * g


def _gelu_tanh(x):
    inner = x * (_GELU_C + (_GELU_C * 0.044715) * (x * x))
    hx = 0.5 * x
    return hx + hx * jnp.tanh(inner)


def _sigmoid(x):
    return 0.5 + 0.5 * jnp.tanh(0.5 * x)


def _softplus(x):
    return jnp.maximum(x, 0.0) + jnp.log1p(jnp.exp(-jnp.abs(x)))


def _shift_rows(x, k, tail):
    xr = pltpu.roll(x, k, 0)
    tr = pltpu.roll(tail, k, 0)
    row = lax.broadcasted_iota(jnp.int32, tail.shape, 0)
    head = jnp.where(row < k, tr, xr[:SUBLANES])
    return jnp.concatenate([head, xr[SUBLANES:]], axis=0)


def _bf16_dot(a, b):
    return jnp.dot(a.astype(jnp.bfloat16), b, preferred_element_type=jnp.float32)


def _load_weights_as_bf16(pairs, stage, sems):
    _, stage_rows, stage_cols = stage.shape
    stage_rows -= stage_rows % (2 * SUBLANES)
    pieces = []
    for src, dst in pairs:
        rows, cols = src.shape
        for r0 in range(0, rows, stage_rows):
            for c0 in range(0, cols, stage_cols):
                pieces.append((src, dst, r0, min(stage_rows, rows - r0), c0, min(stage_cols, cols - c0)))

    def copy(i):
        src, _, r0, nr, c0, nc = pieces[i]
        return pltpu.make_async_copy(src.at[pl.ds(r0, nr), pl.ds(c0, nc)],
                                     stage.at[i % 2, pl.ds(0, nr), pl.ds(0, nc)], sems.at[i % 2])

    copy(0).start()
    for i, (_, dst, r0, nr, c0, nc) in enumerate(pieces):
        if i + 1 < len(pieces):
            copy(i + 1).start()
        copy(i).wait()
        dst[r0:r0 + nr, c0:c0 + nc] = stage[i % 2, :nr, :nc].astype(dst.dtype)


def _block_kernel(
    x_ref, p_ref, g_mix_ref, w_in_hbm, w_conv_lru_ref, b_conv_lru_ref, wa_ref, b_a_ref,
    wx_ref, b_x_ref, lam_ref, w_pool_ref, b_pool_ref, pool_scale_ref, w_out_hbm,
    g_ffn_ref, w_up_hbm, w_conv_ffn_ref, b_conv_ffn_ref, w_down_hbm, g_ple_ref,
    w_ple_gate_hbm, b_ple_gate_ref, w_ple_proj_hbm, g_final_ref,
    out_ref,
    zl_tail, zp_tail, h_state, up_tail, a_buf, b_buf, h_buf, acc_ref, v_buf, up_buf, act_buf,
    w_in_ref, w_out_ref, w_up_ref, w_down_ref, w_ple_gate_ref, w_ple_proj_ref, stage_sems,
):
    ts = x_ref.shape[0]
    n_groups = ts // SUBLANES
    pid = pl.program_id(0)

    @pl.when(pid == 0)
    def _():
        zl_tail[...] = jnp.zeros_like(zl_tail)
        zp_tail[...] = jnp.zeros_like(zp_tail)
        h_state[...] = jnp.zeros_like(h_state)
        up_tail[...] = jnp.zeros_like(up_tail)
        _load_weights_as_bf16(
            [(w_in_hbm, w_in_ref), (w_out_hbm, w_out_ref), (w_up_hbm, w_up_ref),
             (w_down_hbm, w_down_ref), (w_ple_gate_hbm, w_ple_gate_ref),
             (w_ple_proj_hbm, w_ple_proj_ref)],
            up_buf, stage_sems)

    x = x_ref[...]

    u = _rms_norm(x, g_mix_ref[...])
    z = _bf16_dot(u, w_in_ref[...])
    z_lru = z[:, :D_LRU]
    z_gate = z[:, D_LRU:2 * D_LRU]
    z_pool = z[:, 2 * D_LRU:]

    tail = zl_tail[...]
    wc = w_conv_lru_ref[...]
    xc = z_lru * wc[LRU_CONV - 1:LRU_CONV] + b_conv_lru_ref[...]
    for k in range(1, LRU_CONV):
        xc = xc + _shift_rows(z_lru, k, tail) * wc[LRU_CONV - 1 - k:LRU_CONV - k]
    zl_tail[...] = z_lru[ts - SUBLANES:]

    xc_bf = xc.astype(jnp.bfloat16)
    r_parts, i_parts = [], []
    for c in range(D_LRU // MXU_DIM):
        blk = xc_bf[:, c * MXU_DIM:(c + 1) * MXU_DIM]
        r_parts.append(jnp.dot(blk, wa_ref[c], preferred_element_type=jnp.float32))
        i_parts.append(jnp.dot(blk, wx_ref[c], preferred_element_type=jnp.float32))
    r = _sigmoid(jnp.concatenate(r_parts, axis=-1) + b_a_ref[...])
    i_gate = _sigmoid(jnp.concatenate(i_parts, axis=-1) + b_x_ref[...])

    log_a = (-LRU_C * _softplus(-lam_ref[...])) * r
    a = jnp.exp(log_a)
    mult = jnp.sqrt(1.0 - a * a)
    b = mult * (i_gate * xc)

    a3 = a.reshape(n_groups, SUBLANES, D_LRU)
    b3 = b.reshape(n_groups, SUBLANES, D_LRU)
    sub = lax.broadcasted_iota(jnp.int32, (n_groups, SUBLANES, D_LRU), 1)
    d = 1
    while d < SUBLANES:
        keep = sub >= d
        a_s = jnp.where(keep, pltpu.roll(a3, d, 1), 1.0)
        b_s = jnp.where(keep, pltpu.roll(b3, d, 1), 0.0)
        b3 = a3 * b_s + b3
        a3 = a3 * a_s
        d *= 2
    a_buf[...] = a3.reshape(ts, D_LRU)
    b_buf[...] = b3.reshape(ts, D_LRU)

    def group_step(g, h_prev):
        rows = pl.ds(pl.multiple_of(g * SUBLANES, SUBLANES), SUBLANES)
        h_last = jnp.broadcast_to(h_prev[SUBLANES - 1:SUBLANES], (SUBLANES, D_LRU))
        h_new = a_buf[rows, :] * h_last + b_buf[rows, :]
        h_buf[rows, :] = h_new
        return h_new

    h_state[...] = lax.fori_loop(0, n_groups, group_step, h_state[...], unroll=8)
    y_lru = h_buf[...] * _gelu_tanh(z_gate)

    hist = jnp.concatenate([zp_tail[...], z_pool], axis=0)
    zp_tail[...] = z_pool[ts - POOL_HIST:]
    t_glob = pid * ts + lax.broadcasted_iota(jnp.int32, (ts, 1), 0)
    pooled = []
    s = hist
    for gi, w in enumerate(POOL_WINDOWS):
        lo = gi * POOL_GROUP_DIM
        s = s[:, POOL_GROUP_DIM:] if gi > 0 else s
        s = s + pltpu.roll(s, w // 2, 0)
        cnt = jnp.minimum(t_glob + 1, w).astype(jnp.float32)
        win = s[POOL_HIST:, :POOL_GROUP_DIM]
        pooled.append(win * (1.0 / cnt) - z_pool[:, lo:lo + POOL_GROUP_DIM])
    pooled = jnp.concatenate(pooled, axis=-1).astype(jnp.bfloat16)
    yp_parts = []
    for c in range(D_POOL // MXU_DIM):
        yp_parts.append(jnp.dot(pooled[:, c * MXU_DIM:(c + 1) * MXU_DIM], w_pool_ref[c],
                                preferred_element_type=jnp.float32))
    y_pool = (jnp.concatenate(yp_parts, axis=-1) + b_pool_ref[...]) * pool_scale_ref[...]

    mix = jnp.concatenate([y_lru, y_pool], axis=-1)
    h1 = x + _bf16_dot(mix, w_out_ref[...])

    v_buf[...] = _rms_norm(h1, g_ffn_ref[...]).astype(jnp.bfloat16)
    acc_ref[...] = h1

    n_chunks = D_FF // FF_CHUNK

    def up_proj(j):
        for half in range(2):
            cols = slice(half * D_FF + j * FF_CHUNK, half * D_FF + (j + 1) * FF_CHUNK)
            buf_cols = slice(half * FF_CHUNK, (half + 1) * FF_CHUNK)
            up_buf[j % 2, :SUBLANES, buf_cols] = up_tail[:, cols]
            up_buf[j % 2, SUBLANES:, buf_cols] = jnp.dot(
                v_buf[...], w_up_ref[:, cols], preferred_element_type=jnp.float32)

    def down_proj(j):
        acc_ref[...] += jnp.dot(act_buf[j % 2], w_down_ref[j * FF_CHUNK:(j + 1) * FF_CHUNK, :],
                                preferred_element_type=jnp.float32)

    def elementwise(j):
        for r0 in range(0, ts, EW_ROWS):
            cvs = []
            for half in range(2):
                cols = slice(half * D_FF + j * FF_CHUNK, half * D_FF + (j + 1) * FF_CHUNK)
                buf_cols = slice(half * FF_CHUNK, (half + 1) * FF_CHUNK)
                win = up_buf[j % 2, r0:r0 + SUBLANES + EW_ROWS, buf_cols]
                cv = win[SUBLANES:] * w_conv_ffn_ref[FFN_CONV - 1:FFN_CONV, cols] + b_conv_ffn_ref[:, cols]
                for k in range(1, FFN_CONV):
                    shifted = pltpu.roll(win, k, 0)[SUBLANES:]
                    cv = cv + shifted * w_conv_ffn_ref[FFN_CONV - 1 - k:FFN_CONV - k, cols]
                cvs.append(cv)
            act_buf[j % 2, r0:r0 + EW_ROWS, :] = (_gelu_tanh(cvs[0]) * cvs[1]).astype(jnp.bfloat16)
        for half in range(2):
            cols = slice(half * D_FF + j * FF_CHUNK, half * D_FF + (j + 1) * FF_CHUNK)
            up_tail[:, cols] = up_buf[j % 2, ts:, half * FF_CHUNK:(half + 1) * FF_CHUNK]

    for step in range(-1, n_chunks + 1):
        if step + 1 < n_chunks:
            up_proj(step + 1)
        if 0 <= step - 1:
            down_proj(step - 1)
        if 0 <= step < n_chunks:
            elementwise(step)
    h2 = acc_ref[...]

    q = _rms_norm(h2, g_ple_ref[...])
    gate = _sigmoid(_bf16_dot(q, w_ple_gate_ref[...]) + b_ple_gate_ref[...])
    pe = _bf16_dot(p_ref[...], w_ple_proj_ref[...])
    h3 = h2 + gate * pe
    out_ref[...] = _rms_norm(h3, g_final_ref[...])


def _block_diag(w, per_block):
    h, d, _ = w.shape
    w = w.reshape(h // per_block, per_block, d, d)
    eye = jnp.eye(per_block, dtype=w.dtype)
    out = jnp.einsum('bpij,pq->bpiqj', w, eye)
    return out.reshape(h // per_block, per_block * d, per_block * d)


def _resident(shape):
    nd = len(shape)
    return pl.BlockSpec(shape, lambda i: (0,) * nd, pipeline_mode=pl.Buffered(1))


def kernel(x, p, g_mix, w_in, w_conv_lru, b_conv_lru, w_a, b_a, w_x, b_x, lam, w_pool, b_pool, pool_scale, w_out, g_ffn, w_up, w_conv_ffn, b_conv_ffn, w_down, g_ple, w_ple_gate, b_ple_gate, w_ple_proj, g_final):
    bsz, seq, d_model = x.shape
    assert bsz == 1 and d_model == D_MODEL and seq % SEQ_TILE == 0
    assert g_mix.shape[0] == 1
    bf16 = jnp.bfloat16
    row = lambda a: a.reshape(1, -1)

    heads_per_blk = MXU_DIM // LRU_HEAD_DIM
    groups_per_blk = MXU_DIM // POOL_GROUP_DIM
    big = [w_in[0], w_out[0], w_up[0], w_down[0], w_ple_gate[0], w_ple_proj[0]]
    w_in_f, w_out_f, w_up_f, w_down_f, w_ple_gate_f, w_ple_proj_f = big
    operands = [
        x[0], p[0, 0], row(g_mix[0]), w_in_f, w_conv_lru[0], row(b_conv_lru[0]),
        _block_diag(w_a[0], heads_per_blk).astype(bf16), row(b_a[0]),
        _block_diag(w_x[0], heads_per_blk).astype(bf16), row(b_x[0]), row(lam[0]),
        _block_diag(w_pool[0], groups_per_blk).astype(bf16), row(b_pool[0]), row(pool_scale[0]),
        w_out_f, row(g_ffn[0]),
        w_up_f, w_conv_ffn[0], row(b_conv_ffn[0]),
        w_down_f, row(g_ple[0]),
        w_ple_gate_f, row(b_ple_gate[0]), w_ple_proj_f, row(g_final),
    ]
    in_specs = [
        pl.BlockSpec((SEQ_TILE, D_MODEL), lambda i: (i, 0)),
        pl.BlockSpec((SEQ_TILE, D_PLE), lambda i: (i, 0)),
    ] + [pl.BlockSpec(memory_space=pl.ANY) if any(a is w for w in big) else _resident(a.shape)
         for a in operands[2:]]
    f32 = jnp.float32
    scratch = [
        pltpu.VMEM((SUBLANES, D_LRU), f32),
        pltpu.VMEM((POOL_HIST, D_POOL), f32),
        pltpu.VMEM((SUBLANES, D_LRU), f32),
        pltpu.VMEM((SUBLANES, 2 * D_FF), f32),
        pltpu.VMEM((SEQ_TILE, D_LRU), f32),
        pltpu.VMEM((SEQ_TILE, D_LRU), f32),
        pltpu.VMEM((SEQ_TILE, D_LRU), f32),
        pltpu.VMEM((SEQ_TILE, D_MODEL), f32),
        pltpu.VMEM((SEQ_TILE, D_MODEL), bf16),
        pltpu.VMEM((2, SUBLANES + SEQ_TILE, 2 * FF_CHUNK), f32),
        pltpu.VMEM((2, SEQ_TILE, FF_CHUNK), bf16),
    ] + [pltpu.VMEM(w.shape, bf16) for w in big] + [
        pltpu.SemaphoreType.DMA((2,)),
    ]
    out = pl.pallas_call(
        _block_kernel,
        grid=(seq // SEQ_TILE,),
        in_specs=in_specs,
        out_specs=pl.BlockSpec((SEQ_TILE, D_MODEL), lambda i: (i, 0)),
        out_shape=jax.ShapeDtypeStruct((seq, D_MODEL), x.dtype),
        scratch_shapes=scratch,
        compiler_params=pltpu.CompilerParams(
            dimension_semantics=("arbitrary",),
            vmem_limit_bytes=VMEM_LIMIT_BYTES,
        ),
        name="hybrid_block",
    )(*operands)
    return out[None]
```
